```python
import math
import jax, jax.numpy as jnp
from jax import lax
import numpy as np

D_MODEL = 1024
BATCH = 4
SEQ = 4096
DEPTH = 1

POOL_WIDTH = D_MODEL // 2
POOL_WINDOWS = (2, 4, 8, 16)
N_POOL_GROUPS = 4
POOL_GROUP = POOL_WIDTH // N_POOL_GROUPS
HEAD_DIM = 64
N_Q_HEADS = (D_MODEL // 2) // HEAD_DIM
N_KV_HEADS = N_Q_HEADS // 4
Q_PER_KV = N_Q_HEADS // N_KV_HEADS
ATTN_WIDTH = N_Q_HEADS * HEAD_DIM
KV_WIDTH = N_KV_HEADS * HEAD_DIM
WINDOW = 128
BLOCK = 128
N_BUCKETS = 32
MAX_DISTANCE = 128
N_EXPERTS = 32
TOP_K = 4
D_FF = D_MODEL
SWIGLU_LIMIT = 7.0
SWIGLU_ALPHA = 1.702
RMS_EPS = 1e-5
IN_WIDTH = POOL_WIDTH + ATTN_WIDTH + 2 * KV_WIDTH + 2 * D_MODEL
SPLITS = (POOL_WIDTH,
          POOL_WIDTH + ATTN_WIDTH,
          POOL_WIDTH + ATTN_WIDTH + KV_WIDTH,
          POOL_WIDTH + ATTN_WIDTH + 2 * KV_WIDTH,
          POOL_WIDTH + ATTN_WIDTH + 2 * KV_WIDTH + D_MODEL)

kernel_name = "hybrid_pool_swa_sink_moe"


def rmsnorm(x, g):
    xf = x.astype(jnp.float32)
    y = xf * lax.rsqrt(jnp.mean(xf * xf, axis=-1, keepdims=True) + RMS_EPS)
    return (y * g.astype(jnp.float32)).astype(x.dtype)


def t5_bucket(dist):
    max_exact = N_BUCKETS // 2
    d = np.maximum(dist, 1).astype(np.float32)
    large = max_exact + (np.log(d / max_exact) / math.log(MAX_DISTANCE / max_exact)
                         * (N_BUCKETS - max_exact)).astype(np.int32)
    large = np.minimum(large, N_BUCKETS - 1)
    return np.where(dist < max_exact, dist, large).astype(np.int32)


def multiscale_pool(u, pool_w, pool_scale):
    B, S, _ = u.shape
    ug = u.reshape(B, S, N_POOL_GROUPS, POOL_GROUP).astype(jnp.float32)
    cs = jnp.cumsum(ug, axis=1)
    cs0 = jnp.pad(cs, ((0, 0), (1, 0), (0, 0), (0, 0)))
    pos = jnp.arange(1, S + 1, dtype=jnp.float32)
    means = []
    for g, w in enumerate(POOL_WINDOWS):
        lo = jnp.pad(cs0[:, :S + 1 - w, g], ((0, 0), (w - 1, 0), (0, 0)))
        cnt = jnp.minimum(pos, float(w))[None, :, None]
        means.append((cs[:, :, g] - lo) / cnt)
    z = (jnp.stack(means, axis=2) - ug).astype(u.dtype)
    z = jnp.einsum('bsgc,gcd->bsgd', z, pool_w)
    return z.reshape(B, S, POOL_WIDTH) * pool_scale


def sliding_window_attention(q, k, v, sinks, rel_bias):
    B, S, _ = q.shape
    nb = S // BLOCK
    qb = q.reshape(B, nb, BLOCK, N_KV_HEADS, Q_PER_KV, HEAD_DIM)
    k4 = k.reshape(B, S, N_KV_HEADS, HEAD_DIM)
    v4 = v.reshape(B, S, N_KV_HEADS, HEAD_DIM)
    pad = ((0, 0), (BLOCK, 0), (0, 0), (0, 0))
    kp, vp = jnp.pad(k4, pad), jnp.pad(v4, pad)
    blk = (B, nb, BLOCK, N_KV_HEADS, HEAD_DIM)
    kb = jnp.concatenate([kp[:, :S].reshape(blk), k4.reshape(blk)], axis=2)
    vb = jnp.concatenate([vp[:, :S].reshape(blk), v4.reshape(blk)], axis=2)

    qi = np.arange(BLOCK)[:, None]
    kj = np.arange(2 * BLOCK)[None, :]
    dist = BLOCK + qi - kj
    in_band = (dist >= 0) & (dist < WINDOW)
    bucket = t5_bucket(np.clip(dist, 0, None))
    bias = jnp.transpose(rel_bias[bucket].astype(jnp.float32), (2, 0, 1))
    bias = bias.reshape(N_KV_HEADS, Q_PER_KV, BLOCK, 2 * BLOCK)
    kpos = (np.arange(nb)[:, None, None] - 1) * BLOCK + kj[None]
    mask = in_band[None] & (kpos >= 0)

    scores = jnp.einsum('bnqhgd,bnkhd->bnhgqk', qb, kb,
                        preferred_element_type=jnp.float32) * (HEAD_DIM ** -0.5)
    scores = jnp.where(mask[None, :, None, None], scores + bias[None, None], -1e30)
    sink = jnp.broadcast_to(sinks.astype(jnp.float32).reshape(1, 1, N_KV_HEADS, Q_PER_KV, 1, 1),
                            scores.shape[:-1] + (1,))
    probs = jax.nn.softmax(jnp.concatenate([scores, sink], axis=-1), axis=-1)[..., :-1]
    out = jnp.einsum('bnhgqk,bnkhd->bnqhgd', probs.astype(v.dtype), vb)
    return out.reshape(B, S, ATTN_WIDTH)


def clamped_swiglu(hu):
    glu, lin = hu[..., ::2], hu[..., 1::2]
    glu = jnp.minimum(glu, SWIGLU_LIMIT)
    lin = jnp.clip(lin, -SWIGLU_LIMIT, SWIGLU_LIMIT)
    return glu * jax.nn.sigmoid(SWIGLU_ALPHA * glu) * (lin + 1.0)


def moe(h, w_router, b_router, w_up, b_up, w_down, b_down):
    B, S, D = h.shape
    t = h.reshape(-1, D)
    logits = (t @ w_router + b_router).astype(jnp.float32)
    top_vals, top_idx = lax.top_k(logits, TOP_K)
    top_w = jax.nn.softmax(top_vals, axis=-1)
    gates = jnp.einsum('tk,tke->te', top_w, jax.nn.one_hot(top_idx, N_EXPERTS, dtype=jnp.float32))
    out = jnp.zeros((t.shape[0], D), jnp.float32)
    for e in range(N_EXPERTS):
        act = clamped_swiglu(t @ w_up[e] + b_up[e])
        out = out + gates[:, e:e + 1] * (act @ w_down[e] + b_down[e]).astype(jnp.float32)
    return out.astype(h.dtype).reshape(B, S, D)


def setup_inputs(seed: int = 0) -> dict:
    key = jax.random.key(seed)
    ks = jax.random.split(key, 20)
    f32 = jnp.float32
    n = lambda k, shape, s: jax.random.normal(k, shape, f32) * s
    return {
        "x": n(ks[0], (BATCH, SEQ, D_MODEL), 1.0),
        "norm1_g": 1.0 + n(ks[1], (DEPTH, D_MODEL), 0.02),
        "w_in": n(ks[2], (DEPTH, D_MODEL, IN_WIDTH), D_MODEL ** -0.5),
        "pool_w": n(ks[3], (DEPTH, N_POOL_GROUPS, POOL_GROUP, POOL_GROUP), POOL_GROUP ** -0.5),
        "pool_scale": 1.0 + n(ks[4], (DEPTH, POOL_WIDTH), 0.02),
        "w_pool_proj": n(ks[5], (DEPTH, POOL_WIDTH, D_MODEL), POOL_WIDTH ** -0.5),
        "sinks": n(ks[6], (DEPTH, N_Q_HEADS), 1.0),
        "w_attn_proj": n(ks[7], (DEPTH, ATTN_WIDTH, D_MODEL), ATTN_WIDTH ** -0.5),
        "w_out": n(ks[8], (DEPTH, D_MODEL, D_MODEL), D_MODEL ** -0.5),
        "norm2_g": 1.0 + n(ks[9], (DEPTH, D_MODEL), 0.02),
        "w_router": n(ks[10], (DEPTH, D_MODEL, N_EXPERTS), D_MODEL ** -0.5),
        "b_router": n(ks[11], (DEPTH, N_EXPERTS), 0.01),
        "w_up": n(ks[12], (DEPTH, N_EXPERTS, D_MODEL, 2 * D_FF), D_MODEL ** -0.5),
        "b_up": n(ks[13], (DEPTH, N_EXPERTS, 2 * D_FF), 0.01),
        "w_down": n(ks[14], (DEPTH, N_EXPERTS, D_FF, D_MODEL), D_FF ** -0.5),
        "b_down": n(ks[15], (DEPTH, N_EXPERTS, D_MODEL), 0.01),
        "rel_bias": n(ks[16], (N_BUCKETS, N_Q_HEADS), 0.5),
        "final_g": 1.0 + n(ks[17], (D_MODEL,), 0.02),
    }


def reference(x, norm1_g, w_in, pool_w, pool_scale, w_pool_proj, sinks, w_attn_proj, w_out,
              norm2_g, w_router, b_router, w_up, b_up, w_down, b_down, rel_bias, final_g):
    for l in range(DEPTH):
        h = rmsnorm(x, norm1_g[l])
        u, q, k, v, g_pool, g_attn = jnp.split(h @ w_in[l], SPLITS, axis=-1)
        y_pool = multiscale_pool(u, pool_w[l], pool_scale[l]) @ w_pool_proj[l]
        y_attn = sliding_window_attention(q, k, v, sinks[l], rel_bias) @ w_attn_proj[l]
        mixed = jax.nn.sigmoid(g_pool) * y_pool + jax.nn.sigmoid(g_attn) * y_attn
        x = x + mixed @ w_out[l]
        x = x + moe(rmsnorm(x, norm2_g[l]), w_router[l], b_router[l], w_up[l], b_up[l],
                    w_down[l], b_down[l])
    return rmsnorm(x, final_g)
```

```python
import functools
import math

import jax
import jax.numpy as jnp
import numpy as np
from jax import lax
from jax.experimental import pallas as pl
from jax.experimental.pallas import tpu as pltpu

F32 = jnp.float32
BF16 = jnp.bfloat16

D_MODEL = 1024
POOL_WIDTH = 512
POOL_WINDOWS = (2, 4, 8, 16)
POOL_GROUP = 128
HALO = 16
HEAD_DIM = 64
N_Q_HEADS = 8
N_KV_HEADS = 2
Q_PER_KV = 4
ATTN_WIDTH = 512
KV_WIDTH = 128
WINDOW = 128
BLOCK = 128
N_BUCKETS = 32
MAX_DISTANCE = 128
N_EXPERTS = 32
TOP_K = 4
D_FF = 1024
SWIGLU_LIMIT = 7.0
SWIGLU_ALPHA = 1.702
RMS_EPS = 1e-5
NEG = -1e30

LANES = 128
SUBLANES = 8
ROW_CHUNKS = D_MODEL // LANES
VMEM_LIMIT = 56 * 1024 * 1024

T_IN = 512
T_MIX = 512
T_MOE = 256
T_OUT = 256


def _rms(x, g):
    return x * lax.rsqrt(jnp.mean(x * x, axis=-1, keepdims=True) + RMS_EPS) * g


def _inproj_kernel(x_ref, g_ref, w_ref, u_ref, q_ref, kv_ref, gate_ref):
    h = _rms(x_ref[...], g_ref[...]).astype(BF16)
    o = 0
    for ref in (u_ref, q_ref, kv_ref, gate_ref):
        n = ref.shape[-1]
        ref[...] = jnp.dot(h, w_ref[:, o:o + n], preferred_element_type=F32).astype(BF16)
        o += n


def _inproj(x2d, g, w):
    t = x2d.shape[0]
    widths = (POOL_WIDTH, ATTN_WIDTH, 2 * KV_WIDTH, 2 * D_MODEL)
    return pl.pallas_call(
        _inproj_kernel,
        grid=(t // T_IN,),
        in_specs=[pl.BlockSpec((T_IN, D_MODEL), lambda i: (i, 0)),
                  pl.BlockSpec((1, D_MODEL), lambda i: (0, 0)),
                  pl.BlockSpec(w.shape, lambda i: (0, 0))],
        out_specs=[pl.BlockSpec((T_IN, n), lambda i: (i, 0)) for n in widths],
        out_shape=[jax.ShapeDtypeStruct((t, n), BF16) for n in widths],
        compiler_params=pltpu.CompilerParams(dimension_semantics=("arbitrary",),
                                             vmem_limit_bytes=VMEM_LIMIT),
        name="inproj",
    )(x2d, g, w)


def _mixer_kernel(sink_ref, x_ref, u_ref, up_ref, q_ref, kv_ref, kvp_ref, gate_ref,
                  poolw_ref, pscale_ref, wpp_ref, bias_ref, wap_ref, wout_ref,
                  g2_ref, wr_ref, br_ref,
                  x2_ref, h2t_ref, route_ref,
                  ubuf, abuf):
    j = pl.program_id(1)
    ts = x_ref.shape[0]

    ubuf[HALO:, :] = u_ref[...].astype(F32)
    ubuf[:HALO, :] = jnp.where(j > 0, up_ref[...].astype(F32), 0.0)
    pos = (j * ts + lax.broadcasted_iota(jnp.int32, (ts, 1), 0) + 1).astype(F32)
    zs = []
    for g, w in enumerate(POOL_WINDOWS):
        cols = slice(g * POOL_GROUP, (g + 1) * POOL_GROUP)
        cur = ubuf[HALO:, cols]
        acc = cur
        for d in range(1, w):
            acc = acc + ubuf[HALO - d:HALO - d + ts, cols]
        z = (acc / jnp.minimum(pos, float(w)) - cur).astype(BF16)
        zs.append(jnp.dot(z, poolw_ref[g], preferred_element_type=F32))
    zp = (jnp.concatenate(zs, axis=-1) * pscale_ref[...]).astype(BF16)
    y_pool = jnp.dot(zp, wpp_ref[...], preferred_element_type=F32)

    scale = HEAD_DIM ** -0.5
    col = lax.broadcasted_iota(jnp.int32, (BLOCK, 2 * BLOCK), 1)
    for b in range(ts // BLOCK):
        rows = slice(b * BLOCK, (b + 1) * BLOCK)
        if b == 0:
            kvcat = jnp.concatenate([kvp_ref[...], kv_ref[rows, :]], axis=0)
            no_prev = jnp.logical_and(j == 0, col < BLOCK)
        else:
            kvcat = kv_ref[(b - 1) * BLOCK:(b + 1) * BLOCK, :]
            no_prev = None
        outs = []
        for hk in range(N_KV_HEADS):
            k_h = kvcat[:, hk * HEAD_DIM:(hk + 1) * HEAD_DIM]
            v_h = kvcat[:, KV_WIDTH + hk * HEAD_DIM:KV_WIDTH + (hk + 1) * HEAD_DIM]
            for gq in range(Q_PER_KV):
                hq = hk * Q_PER_KV + gq
                q_h = q_ref[rows, hq * HEAD_DIM:(hq + 1) * HEAD_DIM]
                s = lax.dot_general(q_h, k_h, (((1,), (1,)), ((), ())),
                                    preferred_element_type=F32)
                s = s * scale + bias_ref[hq]
                if no_prev is not None:
                    s = jnp.where(no_prev, NEG, s)
                sink = sink_ref[hq]
                m = jnp.maximum(jnp.max(s, axis=-1, keepdims=True), sink)
                p = jnp.exp(s - m)
                denom = jnp.sum(p, axis=-1, keepdims=True) + jnp.exp(sink - m)
                o = jnp.dot(p.astype(BF16), v_h, preferred_element_type=F32)
                outs.append(o / denom)
        abuf[rows, :] = jnp.concatenate(outs, axis=-1).astype(BF16)
    y_attn = jnp.dot(abuf[...], wap_ref[...], preferred_element_type=F32)

    gp = gate_ref[:, :D_MODEL].astype(F32)
    ga = gate_ref[:, D_MODEL:].astype(F32)
    mixed = jax.nn.sigmoid(gp) * y_pool + jax.nn.sigmoid(ga) * y_attn
    x2 = x_ref[...] + jnp.dot(mixed.astype(BF16), wout_ref[...], preferred_element_type=F32)
    x2_ref[...] = x2

    h2 = _rms(x2, g2_ref[...])
    for c in range(ROW_CHUNKS):
        h2t_ref[pl.ds(c, ts, stride=ROW_CHUNKS), :] = h2[:, c * LANES:(c + 1) * LANES]

    logits = jnp.dot(h2, wr_ref[...], preferred_element_type=F32,
                     precision=lax.Precision.HIGHEST) + br_ref[...]
    lane = lax.broadcasted_iota(jnp.int32, logits.shape, 1)
    vals, idxs = [], []
    for _ in range(TOP_K):
        m = jnp.max(logits, axis=-1, keepdims=True)
        idx = jnp.min(jnp.where(logits == m, lane, LANES), axis=-1, keepdims=True)
        vals.append(m)
        idxs.append(idx)
        logits = jnp.where(lane == idx, -jnp.inf, logits)
    es = [jnp.exp(v - vals[0]) for v in vals]
    tot = es[0] + es[1] + es[2] + es[3]
    route = jnp.zeros(lane.shape, F32)
    for k in range(TOP_K):
        route = jnp.where(lane == k, es[k] / tot, route)
        route = jnp.where(lane == TOP_K + k, idxs[k].astype(F32), route)
    route_ref[...] = route


def _mixer(x2d, u, q, kv, gates, sinks, poolw, pscale, wpp, bias, wap, wout, g2, wr, br,
           batch, seq):
    t = x2d.shape[0]
    ns = seq // T_MIX
    tile = lambda b, j: (b * ns + j, 0)
    full2 = lambda b, j: (0, 0)
    full3 = lambda b, j: (0, 0, 0)
    halo_blocks = T_MIX // HALO
    kv_blocks = T_MIX // BLOCK
    prev_halo = lambda b, j: (jnp.maximum((b * ns + j) * halo_blocks - 1, 0), 0)
    prev_kv = lambda b, j: (jnp.maximum((b * ns + j) * kv_blocks - 1, 0), 0)
    return pl.pallas_call(
        _mixer_kernel,
        grid=(batch, ns),
        in_specs=[pl.BlockSpec(memory_space=pltpu.SMEM),
                  pl.BlockSpec((T_MIX, D_MODEL), tile),
                  pl.BlockSpec((T_MIX, POOL_WIDTH), tile),
                  pl.BlockSpec((HALO, POOL_WIDTH), prev_halo),
                  pl.BlockSpec((T_MIX, ATTN_WIDTH), tile),
                  pl.BlockSpec((T_MIX, 2 * KV_WIDTH), tile),
                  pl.BlockSpec((BLOCK, 2 * KV_WIDTH), prev_kv),
                  pl.BlockSpec((T_MIX, 2 * D_MODEL), tile),
                  pl.BlockSpec(poolw.shape, full3),
                  pl.BlockSpec(pscale.shape, full2),
                  pl.BlockSpec(wpp.shape, full2),
                  pl.BlockSpec(bias.shape, full3),
                  pl.BlockSpec(wap.shape, full2),
                  pl.BlockSpec(wout.shape, full2),
                  pl.BlockSpec(g2.shape, full2),
                  pl.BlockSpec(wr.shape, full2),
                  pl.BlockSpec(br.shape, full2)],
        out_specs=[pl.BlockSpec((T_MIX, D_MODEL), tile),
                   pl.BlockSpec((T_MIX * ROW_CHUNKS, LANES), tile),
                   pl.BlockSpec((T_MIX, LANES), tile)],
        out_shape=[jax.ShapeDtypeStruct((t, D_MODEL), F32),
                   jax.ShapeDtypeStruct((t * ROW_CHUNKS, LANES), F32),
                   jax.ShapeDtypeStruct((t, LANES), F32)],
        scratch_shapes=[pltpu.VMEM((T_MIX + HALO, POOL_WIDTH), F32),
                        pltpu.VMEM((T_MIX, ATTN_WIDTH), BF16)],
        compiler_params=pltpu.CompilerParams(dimension_semantics=("arbitrary", "arbitrary"),
                                             vmem_limit_bytes=VMEM_LIMIT),
        name="mixer",
    )(sinks, x2d, u, u, q, kv, kv, gates, poolw, pscale, wpp, bias, wap, wout, g2, wr, br)


def _moe_kernel(te_ref, nu_ref,
                src0_ref, src1_ref, dst_ref,
                h2t_hbm, wup_ref, bup_ref, wdn_ref, bdn_ref, perm_ref,
                y_hbm,
                xbuf, ybuf, wup_b, wdn_b, gsem, ssem):
    i = pl.program_id(0)
    n_used = nu_ref[0]
    tm = T_MOE
    slot = lax.rem(i, 2)

    def gather_copy(tok, r, s):
        return pltpu.make_async_copy(
            h2t_hbm.at[pl.ds(pl.multiple_of(tok * ROW_CHUNKS, ROW_CHUNKS), ROW_CHUNKS), :],
            xbuf.at[s, pl.ds(pl.multiple_of(r * ROW_CHUNKS, ROW_CHUNKS), ROW_CHUNKS), :],
            gsem.at[s])

    def scatter_copy(d, r, s):
        return pltpu.make_async_copy(
            ybuf.at[s, pl.ds(pl.multiple_of(r * ROW_CHUNKS, ROW_CHUNKS), ROW_CHUNKS), :],
            y_hbm.at[pl.ds(pl.multiple_of(d * ROW_CHUNKS, ROW_CHUNKS), ROW_CHUNKS), :],
            ssem.at[s])

    def start_gather(src_ref, s):
        def body(r, c):
            gather_copy(src_ref[0, 0, r], r, s).start()
            return c
        lax.fori_loop(0, tm, body, 0, unroll=8)

    def wait_gather(s):
        def body(r, c):
            gather_copy(0, r, s).wait()
            return c
        lax.fori_loop(0, tm, body, 0, unroll=8)

    def start_scatter(s):
        def body(r, c):
            scatter_copy(dst_ref[0, 0, r], r, s).start()
            return c
        lax.fori_loop(0, tm, body, 0, unroll=8)

    def wait_scatter(s):
        def body(r, c):
            scatter_copy(0, r, s).wait()
            return c
        lax.fori_loop(0, tm, body, 0, unroll=8)

    @pl.when(i < n_used)
    def _():
        @pl.when(i == 0)
        def _():
            start_gather(src0_ref, 0)
            ybuf[1] = jnp.zeros(ybuf.shape[1:], F32)
            n_rows = tm * ROW_CHUNKS
            for half in range(2):
                fill = pltpu.make_async_copy(
                    ybuf.at[1],
                    y_hbm.at[pl.ds(y_hbm.shape[0] - (2 - half) * n_rows, n_rows), :],
                    ssem.at[1])
                fill.start()
                fill.wait()

        @pl.when(i + 1 < n_used)
        def _():
            start_gather(src1_ref, 1 - slot)

        prev = te_ref[jnp.maximum(i - 1, 0)]

        @pl.when(jnp.logical_or(i == 0, te_ref[i] != prev))
        def _():
            for c in range(2 * D_FF // 256):
                cols = slice(c * 256, (c + 1) * 256)
                wup_b[:, cols] = jnp.dot(wup_ref[:, cols].astype(BF16), perm_ref[...],
                                         preferred_element_type=F32).astype(BF16)
            wdn_b[...] = wdn_ref[...].astype(BF16)

        wait_gather(slot)
        xs = xbuf.at[slot]
        x = jnp.concatenate(
            [xs[pl.ds(c, tm, stride=ROW_CHUNKS), :] for c in range(ROW_CHUNKS)],
            axis=-1).astype(BF16)
        hu = jnp.dot(x, wup_b[...], preferred_element_type=F32) + bup_ref[...]
        acts = []
        for c in range(2 * D_FF // 256):
            glu = jnp.minimum(hu[:, c * 256:c * 256 + LANES], SWIGLU_LIMIT)
            lin = jnp.clip(hu[:, c * 256 + LANES:(c + 1) * 256], -SWIGLU_LIMIT, SWIGLU_LIMIT)
            acts.append(glu * jax.nn.sigmoid(SWIGLU_ALPHA * glu) * (lin + 1.0))
        act = jnp.concatenate(acts, axis=-1).astype(BF16)
        y = jnp.dot(act, wdn_b[...], preferred_element_type=F32) + bdn_ref[...]

        @pl.when(i >= 2)
        def _():
            wait_scatter(slot)

        ys = ybuf.at[slot]
        for c in range(ROW_CHUNKS):
            ys[pl.ds(c, tm, stride=ROW_CHUNKS), :] = y[:, c * LANES:(c + 1) * LANES]
        start_scatter(slot)

        @pl.when(i == n_used - 1)
        def _():
            wait_scatter(slot)

            @pl.when(i >= 1)
            def _():
                wait_scatter(1 - slot)


def _moe(h2t, src, dst, tile_expert, n_used, w_up, b_up_p, w_down, b_down, perm, n_tiles, y_rows):
    tm = T_MOE
    row_tab = lambda off: pl.BlockSpec(
        (1, 1, tm), lambda i, te, nu: (jnp.minimum(i + off, n_tiles - 1), 0, 0),
        memory_space=pltpu.SMEM)
    by_expert3 = lambda i, te, nu: (te[i], 0, 0)
    grid_spec = pltpu.PrefetchScalarGridSpec(
        num_scalar_prefetch=2,
        grid=(n_tiles,),
        in_specs=[row_tab(0), row_tab(1), row_tab(0),
                  pl.BlockSpec(memory_space=pl.ANY),
                  pl.BlockSpec((None, D_MODEL, 2 * D_FF), by_expert3),
                  pl.BlockSpec((None, 1, 2 * D_FF), by_expert3),
                  pl.BlockSpec((None, D_FF, D_MODEL), by_expert3),
                  pl.BlockSpec((None, 1, D_MODEL), by_expert3),
                  pl.BlockSpec(perm.shape, lambda i, te, nu: (0, 0))],
        out_specs=pl.BlockSpec(memory_space=pl.ANY),
        scratch_shapes=[pltpu.VMEM((2, tm * ROW_CHUNKS, LANES), F32),
                        pltpu.VMEM((2, tm * ROW_CHUNKS, LANES), F32),
                        pltpu.VMEM((D_MODEL, 2 * D_FF), BF16),
                        pltpu.VMEM((D_FF, D_MODEL), BF16),
                        pltpu.SemaphoreType.DMA((2,)),
                        pltpu.SemaphoreType.DMA((2,))])
    return pl.pallas_call(
        _moe_kernel,
        grid_spec=grid_spec,
        out_shape=jax.ShapeDtypeStruct((y_rows, LANES), F32),
        compiler_params=pltpu.CompilerParams(dimension_semantics=("arbitrary",),
                                             vmem_limit_bytes=VMEM_LIMIT),
        name="moe",
    )(tile_expert, n_used, src, src, dst, h2t, w_up, b_up_p, w_down, b_down, perm)


def _combine_kernel(x2_ref, route_ref, y0_ref, y1_ref, y2_ref, y3_ref, g_ref, o_ref):
    tt = x2_ref.shape[0]
    gates = [route_ref[:, k:k + 1] for k in range(TOP_K)]
    chunks = []
    ss = jnp.zeros((tt, 1), F32)
    for c in range(ROW_CHUNKS):
        acc = x2_ref[:, c * LANES:(c + 1) * LANES]
        for k, y_ref in enumerate((y0_ref, y1_ref, y2_ref, y3_ref)):
            acc = acc + gates[k] * y_ref[pl.ds(c, tt, stride=ROW_CHUNKS), :]
        chunks.append(acc)
        ss = ss + jnp.sum(acc * acc, axis=-1, keepdims=True)
    inv = lax.rsqrt(ss / D_MODEL + RMS_EPS)
    for c in range(ROW_CHUNKS):
        o_ref[:, c * LANES:(c + 1) * LANES] = chunks[c] * inv * g_ref[:, c * LANES:(c + 1) * LANES]


def _combine(x2, route, y, g):
    t = x2.shape[0]
    nt = t // T_OUT
    plane = lambda k: pl.BlockSpec((T_OUT * ROW_CHUNKS, LANES), lambda i: (k * nt + i, 0))
    return pl.pallas_call(
        _combine_kernel,
        grid=(nt,),
        in_specs=[pl.BlockSpec((T_OUT, D_MODEL), lambda i: (i, 0)),
                  pl.BlockSpec((T_OUT, LANES), lambda i: (i, 0)),
                  plane(0), plane(1), plane(2), plane(3),
                  pl.BlockSpec((1, D_MODEL), lambda i: (0, 0))],
        out_specs=pl.BlockSpec((T_OUT, D_MODEL), lambda i: (i, 0)),
        out_shape=jax.ShapeDtypeStruct((t, D_MODEL), F32),
        compiler_params=pltpu.CompilerParams(dimension_semantics=("arbitrary",),
                                             vmem_limit_bytes=VMEM_LIMIT),
        name="combine",
    )(x2, route, y, y, y, y, g)


def _t5_bucket(dist):
    max_exact = N_BUCKETS // 2
    d = np.maximum(dist, 1).astype(np.float32)
    large = max_exact + (np.log(d / max_exact) / math.log(MAX_DISTANCE / max_exact)
                         * (N_BUCKETS - max_exact)).astype(np.int32)
    large = np.minimum(large, N_BUCKETS - 1)
    return np.where(dist < max_exact, dist, large).astype(np.int32)


def _band_tables():
    qi = np.arange(BLOCK)[:, None]
    kj = np.arange(2 * BLOCK)[None, :]
    dist = BLOCK + qi - kj
    in_band = (dist >= 0) & (dist < WINDOW)
    return _t5_bucket(np.clip(dist, 0, None)), in_band


def _glu_perm():
    p = np.zeros((256, 256), np.float32)
    c = np.arange(LANES)
    p[2 * c, c] = 1.0
    p[2 * c + 1, LANES + c] = 1.0
    return p


def _routing_tables(eid, t, n_tiles):
    tm = T_MOE
    n_pairs = t * TOP_K
    flat = eid.reshape(-1)
    skey = jnp.sort(flat * n_pairs + jnp.arange(n_pairs, dtype=jnp.int32))
    counts = jnp.sum((flat[:, None] == jnp.arange(N_EXPERTS, dtype=jnp.int32)[None, :])
                     .astype(jnp.int32), axis=0)
    tiles_e = (counts + tm - 1) // tm
    tile_end = jnp.cumsum(tiles_e)
    tile_start = tile_end - tiles_e
    row_start = jnp.cumsum(counts) - counts
    n_used = tile_end[-1]
    ti = jnp.arange(n_tiles, dtype=jnp.int32)
    te_raw = jnp.sum((ti[:, None] >= tile_end[None, :]).astype(jnp.int32), axis=1)
    te_last = jnp.sum(((n_used - 1) >= tile_end).astype(jnp.int32))
    te = jnp.minimum(te_raw, te_last).astype(jnp.int32)
    local = (ti - tile_start[te])[:, None] * tm + jnp.arange(tm, dtype=jnp.int32)[None, :]
    valid = (local < counts[te][:, None]) & (ti < n_used)[:, None]
    sidx = jnp.clip(row_start[te][:, None] + local, 0, n_pairs - 1)
    pair = skey[sidx] % n_pairs
    tok = pair // TOP_K
    k = pair % TOP_K
    src = jnp.where(valid, tok, 0).astype(jnp.int32)
    discard = n_pairs + (ti % 2)[:, None] * tm + jnp.arange(tm, dtype=jnp.int32)[None, :]
    dst = jnp.where(valid, k * t + tok, discard).astype(jnp.int32)
    return (src.reshape(n_tiles, 1, tm), dst.reshape(n_tiles, 1, tm), te,
            n_used.reshape(1).astype(jnp.int32))


def kernel(x, norm1_g, w_in, pool_w, pool_scale, w_pool_proj, sinks, w_attn_proj, w_out, norm2_g,
           w_router, b_router, w_up, b_up, w_down, b_down, rel_bias, final_g):
    batch, seq, d = x.shape
    t = batch * seq
    assert w_in.shape[0] == 1, "single-layer block"
    l = 0
    bucket, in_band = _band_tables()
    perm = jnp.asarray(_glu_perm(), BF16)
    n_tiles = t * TOP_K // T_MOE + N_EXPERTS
    y_rows = (TOP_K * t + 2 * T_MOE) * ROW_CHUNKS

    x2d = x.reshape(t, d)
    u, q, kv, gates = _inproj(x2d, norm1_g[l][None, :], w_in[l].astype(BF16))

    bias = jnp.transpose(rel_bias[bucket].astype(F32), (2, 0, 1))
    bias = jnp.where(in_band[None], bias, NEG)
    wr = jnp.pad(w_router[l], ((0, 0), (0, LANES - N_EXPERTS)))
    br = jnp.pad(b_router[l], (0, LANES - N_EXPERTS), constant_values=-jnp.inf)[None, :]
    x2, h2t, route = _mixer(
        x2d, u, q, kv, gates, sinks[l], pool_w[l].astype(BF16), pool_scale[l][None, :],
        w_pool_proj[l].astype(BF16), bias, w_attn_proj[l].astype(BF16),
        w_out[l].astype(BF16), norm2_g[l][None, :], wr, br, batch, seq)

    eid = route[:, TOP_K:2 * TOP_K].astype(jnp.int32)
    src, dst, te, n_used = _routing_tables(eid, t, n_tiles)
    b_up_p = (b_up[l].reshape(N_EXPERTS, 2 * D_FF // 256, LANES, 2)
              .transpose(0, 1, 3, 2).reshape(N_EXPERTS, 1, 2 * D_FF))
    y = _moe(h2t, src, dst, te, n_used, w_up[l], b_up_p, w_down[l],
             b_down[l][:, None, :], perm, n_tiles, y_rows)

    out = _combine(x2, route, y, final_g[None, :])
    return out.reshape(batch, seq, d)
```

```python
import math

import jax
import jax.numpy as jnp
import numpy as np
from jax import lax
from jax.experimental import pallas as pl
from jax.experimental.pallas import tpu as pltpu

F32 = jnp.float32
BF16 = jnp.bfloat16

D_MODEL = 1024
POOL_WIDTH = 512
POOL_WINDOWS = (2, 4, 8, 16)
POOL_GROUP = 128
HALO = 16
HEAD_DIM = 64
N_Q_HEADS = 8
N_KV_HEADS = 2
Q_PER_KV = 4
ATTN_WIDTH = 512
KV_WIDTH = 128
WINDOW = 128
BLOCK = 128
N_BUCKETS = 32
MAX_DISTANCE = 128
N_EXPERTS = 32
TOP_K = 4
D_FF = 1024
SWIGLU_LIMIT = 7.0
SWIGLU_ALPHA = 1.702
RMS_EPS = 1e-5
NEG = -1e30

LANES = 128
SUBLANES = 8
ROW_CHUNKS = D_MODEL // LANES
VMEM_LIMIT = 56 * 1024 * 1024

T_IN = 512
T_MIX = 512
T_MOE = 256
RING = 3
T_OUT = 256


def _rms(x, g):
    return x * lax.rsqrt(jnp.mean(x * x, axis=-1, keepdims=True) + RMS_EPS) * g


def _inproj_kernel(x_ref, g_ref, w_ref, u_ref, q_ref, kv_ref, gate_ref):
    h = _rms(x_ref[...], g_ref[...]).astype(BF16)
    o = 0
    for ref in (u_ref, q_ref, kv_ref, gate_ref):
        n = ref.shape[-1]
        ref[...] = jnp.dot(h, w_ref[:, o:o + n], preferred_element_type=F32).astype(BF16)
        o += n


def _inproj(x2d, g, w):
    t = x2d.shape[0]
    widths = (POOL_WIDTH, ATTN_WIDTH, 2 * KV_WIDTH, 2 * D_MODEL)
    return pl.pallas_call(
        _inproj_kernel,
        grid=(t // T_IN,),
        in_specs=[pl.BlockSpec((T_IN, D_MODEL), lambda i: (i, 0)),
                  pl.BlockSpec((1, D_MODEL), lambda i: (0, 0)),
                  pl.BlockSpec(w.shape, lambda i: (0, 0))],
        out_specs=[pl.BlockSpec((T_IN, n), lambda i: (i, 0)) for n in widths],
        out_shape=[jax.ShapeDtypeStruct((t, n), BF16) for n in widths],
        compiler_params=pltpu.CompilerParams(dimension_semantics=("arbitrary",),
                                             vmem_limit_bytes=VMEM_LIMIT),
        name="inproj",
    )(x2d, g, w)


def _mixer_kernel(sink_ref, x_ref, u_ref, up_ref, q_ref, kv_ref, kvp_ref, gate_ref,
                  poolw_ref, pscale_ref, wpp_ref, bias_ref, wap_ref, wout_ref,
                  g2_ref, wr_ref, br_ref,
                  x2_ref, h2t_ref, route_ref,
                  ubuf, abuf):
    j = pl.program_id(1)
    ts = x_ref.shape[0]

    ubuf[HALO:, :] = u_ref[...].astype(F32)
    ubuf[:HALO, :] = jnp.where(j > 0, up_ref[...].astype(F32), 0.0)
    pos = (j * ts + lax.broadcasted_iota(jnp.int32, (ts, 1), 0) + 1).astype(F32)
    zs = []
    for g, w in enumerate(POOL_WINDOWS):
        cols = slice(g * POOL_GROUP, (g + 1) * POOL_GROUP)
        cur = ubuf[HALO:, cols]
        acc = cur
        for d in range(1, w):
            acc = acc + ubuf[HALO - d:HALO - d + ts, cols]
        z = (acc / jnp.minimum(pos, float(w)) - cur).astype(BF16)
        zs.append(jnp.dot(z, poolw_ref[g], preferred_element_type=F32))
    zp = (jnp.concatenate(zs, axis=-1) * pscale_ref[...]).astype(BF16)
    y_pool = jnp.dot(zp, wpp_ref[...], preferred_element_type=F32)

    scale = HEAD_DIM ** -0.5
    col = lax.broadcasted_iota(jnp.int32, (BLOCK, 2 * BLOCK), 1)
    for b in range(ts // BLOCK):
        rows = slice(b * BLOCK, (b + 1) * BLOCK)
        if b == 0:
            kvcat = jnp.concatenate([kvp_ref[...], kv_ref[rows, :]], axis=0)
            no_prev = jnp.logical_and(j == 0, col < BLOCK)
        else:
            kvcat = kv_ref[(b - 1) * BLOCK:(b + 1) * BLOCK, :]
            no_prev = None
        outs = []
        for hk in range(N_KV_HEADS):
            k_h = kvcat[:, hk * HEAD_DIM:(hk + 1) * HEAD_DIM]
            v_h = kvcat[:, KV_WIDTH + hk * HEAD_DIM:KV_WIDTH + (hk + 1) * HEAD_DIM]
            for gq in range(Q_PER_KV):
                hq = hk * Q_PER_KV + gq
                q_h = q_ref[rows, hq * HEAD_DIM:(hq + 1) * HEAD_DIM]
                s = lax.dot_general(q_h, k_h, (((1,), (1,)), ((), ())),
                                    preferred_element_type=F32)
                s = s * scale + bias_ref[hq]
                if no_prev is not None:
                    s = jnp.where(no_prev, NEG, s)
                sink = sink_ref[hq]
                m = jnp.maximum(jnp.max(s, axis=-1, keepdims=True), sink)
                p = jnp.exp(s - m)
                denom = jnp.sum(p, axis=-1, keepdims=True) + jnp.exp(sink - m)
                o = jnp.dot(p.astype(BF16), v_h, preferred_element_type=F32)
                outs.append(o / denom)
        abuf[rows, :] = jnp.concatenate(outs, axis=-1).astype(BF16)
    y_attn = jnp.dot(abuf[...], wap_ref[...], preferred_element_type=F32)

    gp = gate_ref[:, :D_MODEL].astype(F32)
    ga = gate_ref[:, D_MODEL:].astype(F32)
    mixed = jax.nn.sigmoid(gp) * y_pool + jax.nn.sigmoid(ga) * y_attn
    x2 = x_ref[...] + jnp.dot(mixed.astype(BF16), wout_ref[...], preferred_element_type=F32)
    x2_ref[...] = x2

    h2 = _rms(x2, g2_ref[...])
    for c in range(ROW_CHUNKS):
        h2t_ref[pl.ds(c, ts, stride=ROW_CHUNKS), :] = h2[:, c * LANES:(c + 1) * LANES]

    hi = h2.astype(BF16)
    lo = (h2 - hi.astype(F32)).astype(BF16)
    logits = jnp.dot(jnp.concatenate([hi, lo, hi], axis=-1), wr_ref[...],
                     preferred_element_type=F32) + br_ref[...]

    lane = lax.broadcasted_iota(jnp.int32, logits.shape, 1)
    vals, idxs = [], []
    for _ in range(TOP_K):
        m = jnp.max(logits, axis=-1, keepdims=True)
        idx = jnp.min(jnp.where(logits == m, lane, LANES), axis=-1, keepdims=True)
        vals.append(m)
        idxs.append(idx)
        logits = jnp.where(lane == idx, -jnp.inf, logits)
    es = [jnp.exp(v - vals[0]) for v in vals]
    tot = es[0] + es[1] + es[2] + es[3]
    route = jnp.zeros(lane.shape, F32)
    for k in range(TOP_K):
        route = jnp.where(lane == k, es[k] / tot, route)
        route = jnp.where(lane == TOP_K + k, idxs[k].astype(F32), route)
    route_ref[...] = route


def _mixer(x2d, u, q, kv, gates, sinks, poolw, pscale, wpp, bias, wap, wout, g2, wr, br,
           batch, seq):
    t = x2d.shape[0]
    ns = seq // T_MIX
    tile = lambda b, j: (b * ns + j, 0)
    full2 = lambda b, j: (0, 0)
    full3 = lambda b, j: (0, 0, 0)
    halo_blocks = T_MIX // HALO
    kv_blocks = T_MIX // BLOCK
    prev_halo = lambda b, j: (jnp.maximum((b * ns + j) * halo_blocks - 1, 0), 0)
    prev_kv = lambda b, j: (jnp.maximum((b * ns + j) * kv_blocks - 1, 0), 0)
    return pl.pallas_call(
        _mixer_kernel,
        grid=(batch, ns),
        in_specs=[pl.BlockSpec(memory_space=pltpu.SMEM),
                  pl.BlockSpec((T_MIX, D_MODEL), tile),
                  pl.BlockSpec((T_MIX, POOL_WIDTH), tile),
                  pl.BlockSpec((HALO, POOL_WIDTH), prev_halo),
                  pl.BlockSpec((T_MIX, ATTN_WIDTH), tile),
                  pl.BlockSpec((T_MIX, 2 * KV_WIDTH), tile),
                  pl.BlockSpec((BLOCK, 2 * KV_WIDTH), prev_kv),
                  pl.BlockSpec((T_MIX, 2 * D_MODEL), tile),
                  pl.BlockSpec(poolw.shape, full3),
                  pl.BlockSpec(pscale.shape, full2),
                  pl.BlockSpec(wpp.shape, full2),
                  pl.BlockSpec(bias.shape, full3),
                  pl.BlockSpec(wap.shape, full2),
                  pl.BlockSpec(wout.shape, full2),
                  pl.BlockSpec(g2.shape, full2),
                  pl.BlockSpec(wr.shape, full2),
                  pl.BlockSpec(br.shape, full2)],
        out_specs=[pl.BlockSpec((T_MIX, D_MODEL), tile),
                   pl.BlockSpec((T_MIX * ROW_CHUNKS, LANES), tile),
                   pl.BlockSpec((T_MIX, LANES), tile)],
        out_shape=[jax.ShapeDtypeStruct((t, D_MODEL), F32),
                   jax.ShapeDtypeStruct((t * ROW_CHUNKS, LANES), F32),
                   jax.ShapeDtypeStruct((t, LANES), F32)],
        scratch_shapes=[pltpu.VMEM((T_MIX + HALO, POOL_WIDTH), F32),
                        pltpu.VMEM((T_MIX, ATTN_WIDTH), BF16)],
        compiler_params=pltpu.CompilerParams(dimension_semantics=("arbitrary", "arbitrary"),
                                             vmem_limit_bytes=VMEM_LIMIT),
        name="mixer",
    )(sinks, x2d, u, u, q, kv, kv, gates, poolw, pscale, wpp, bias, wap, wout, g2, wr, br)


def _moe_kernel(te_ref, nxt_ref, par_ref, nu_ref,
                src_head_ref, src_ahead_ref, dst_prev_ref, dst_own_ref,
                h2t_hbm, wup_hbm, wdn_hbm, bup_ref, bdn_ref, perm_ref,
                y_hbm,
                xb0, xb1, xb2, yb0, yb1, yb2, wup_f, wdn_f, wup_b, wdn_b, gsem, ssem, wsem):
    g = pl.program_id(0)
    n_used = nu_ref[0]
    tm = T_MOE
    n_rows = tm * ROW_CHUNKS
    xbs = (xb0, xb1, xb2)
    ybs = (yb0, yb1, yb2)

    def row(r):
        return pl.ds(pl.multiple_of(r * ROW_CHUNKS, ROW_CHUNKS), ROW_CHUNKS)

    def gather_copy(tok, r, s):
        return pltpu.make_async_copy(h2t_hbm.at[row(tok), :], xbs[s].at[row(r), :], gsem.at[s])

    def scatter_copy(d, r, s):
        return pltpu.make_async_copy(ybs[s].at[row(r), :], y_hbm.at[row(d), :], ssem.at[s])

    def wait_gather(s):
        pltpu.make_async_copy(h2t_hbm.at[pl.ds(0, n_rows), :], xbs[s], gsem.at[s]).wait()

    def wait_scatter(s):
        pltpu.make_async_copy(ybs[s], y_hbm.at[pl.ds(0, n_rows), :], ssem.at[s]).wait()

    def weight_copies(e, p):
        return (pltpu.make_async_copy(wup_hbm.at[e], wup_f.at[p], wsem.at[p]),
                pltpu.make_async_copy(wdn_hbm.at[e], wdn_f.at[p], wsem.at[p]))

    @pl.when(g == 0)
    def _():
        for cp in weight_copies(te_ref[0], par_ref[0]):
            cp.start()
        for s in range(2):
            def body(r, c, s=s):
                gather_copy(src_head_ref[s, 0, r], r, s).start()
                return c
            lax.fori_loop(0, tm, body, 0, unroll=8)
        yb2[...] = jnp.zeros(yb2.shape, F32)
        for part in range(RING):
            fill = pltpu.make_async_copy(
                yb2, y_hbm.at[pl.ds(y_hbm.shape[0] - (RING - part) * n_rows, n_rows), :],
                ssem.at[2])
            fill.start()
            fill.wait()

    for a in range(RING):
        i = g * RING + a
        ahead = (a + 2) % RING
        behind = (a + 1) % RING

        @pl.when(i < n_used)
        def _(a=a, i=i, ahead=ahead, behind=behind):
            e = te_ref[i]

            @pl.when(jnp.logical_or(i == 0, e != te_ref[jnp.maximum(i - 1, 0)]))
            def _():
                p = par_ref[i]
                for cp in weight_copies(e, p):
                    cp.wait()
                for c in range(2 * D_FF // 256):
                    cols = slice(c * 256, (c + 1) * 256)
                    wup_b[:, cols] = jnp.dot(wup_f[p, :, cols].astype(BF16), perm_ref[...],
                                             preferred_element_type=F32).astype(BF16)
                wdn_b[...] = wdn_f[p].astype(BF16)

                @pl.when(nxt_ref[i] != e)
                def _():
                    for cp in weight_copies(nxt_ref[i], 1 - p):
                        cp.start()

            wait_gather(a)

            @pl.when(i >= 2)
            def _():
                wait_scatter(a)

            for r in range(tm):
                gather_copy(src_ahead_ref[a, 0, r], r, ahead).start(priority=r % 2)
            for r in range(tm):
                scatter_copy(dst_prev_ref[a, 0, r], r, ahead).start(priority=r % 2)

            x = jnp.concatenate(
                [xbs[a][pl.ds(c, tm, stride=ROW_CHUNKS), :] for c in range(ROW_CHUNKS)],
                axis=-1).astype(BF16)
            hu = jnp.dot(x, wup_b[...], preferred_element_type=F32) + bup_ref[pl.ds(e, 1), :]
            acts = []
            for c in range(2 * D_FF // 256):
                glu = jnp.minimum(hu[:, c * 256:c * 256 + LANES], SWIGLU_LIMIT)
                lin = jnp.clip(hu[:, c * 256 + LANES:(c + 1) * 256],
                               -SWIGLU_LIMIT, SWIGLU_LIMIT)
                acts.append(glu * jax.nn.sigmoid(SWIGLU_ALPHA * glu) * (lin + 1.0))
            act = jnp.concatenate(acts, axis=-1).astype(BF16)
            y = jnp.dot(act, wdn_b[...], preferred_element_type=F32) + bdn_ref[pl.ds(e, 1), :]
            for c in range(ROW_CHUNKS):
                ybs[a][pl.ds(c, tm, stride=ROW_CHUNKS), :] = y[:, c * LANES:(c + 1) * LANES]

            @pl.when(i == n_used - 1)
            def _():
                def body(r, c):
                    scatter_copy(dst_own_ref[a, 0, r], r, a).start()
                    return c
                lax.fori_loop(0, tm, body, 0, unroll=8)
                wait_scatter(a)
                wait_scatter(ahead)

                @pl.when(i >= 1)
                def _():
                    wait_scatter(behind)
                wait_gather(behind)
                wait_gather(ahead)


def _moe(h2t, tables, w_up, b_up_p, w_down, b_down, perm, n_tiles, y_rows):
    tm = T_MOE
    src_ahead, dst_prev, dst_own, src_head, te, nxt, par, n_used = tables
    smem_rows = lambda n, imap: pl.BlockSpec((n, 1, tm), imap, memory_space=pltpu.SMEM)
    by_step = lambda g, *_: (g, 0, 0)
    const2 = lambda g, *_: (0, 0)
    grid_spec = pltpu.PrefetchScalarGridSpec(
        num_scalar_prefetch=4,
        grid=(n_tiles // RING,),
        in_specs=[smem_rows(2, lambda g, *_: (0, 0, 0)),
                  smem_rows(RING, by_step), smem_rows(RING, by_step), smem_rows(RING, by_step),
                  pl.BlockSpec(memory_space=pl.ANY),
                  pl.BlockSpec(memory_space=pl.ANY),
                  pl.BlockSpec(memory_space=pl.ANY),
                  pl.BlockSpec(b_up_p.shape, const2),
                  pl.BlockSpec(b_down.shape, const2),
                  pl.BlockSpec(perm.shape, const2)],
        out_specs=pl.BlockSpec(memory_space=pl.ANY),
        scratch_shapes=([pltpu.VMEM((tm * ROW_CHUNKS, LANES), F32)] * (2 * RING)
                        + [pltpu.VMEM((2, D_MODEL, 2 * D_FF), F32),
                           pltpu.VMEM((2, D_FF, D_MODEL), F32),
                           pltpu.VMEM((D_MODEL, 2 * D_FF), BF16),
                           pltpu.VMEM((D_FF, D_MODEL), BF16),
                           pltpu.SemaphoreType.DMA((RING,)),
                           pltpu.SemaphoreType.DMA((RING,)),
                           pltpu.SemaphoreType.DMA((2,))]))
    return pl.pallas_call(
        _moe_kernel,
        grid_spec=grid_spec,
        out_shape=jax.ShapeDtypeStruct((y_rows, LANES), F32),
        compiler_params=pltpu.CompilerParams(dimension_semantics=("arbitrary",),
                                             vmem_limit_bytes=VMEM_LIMIT),
        name="moe",
    )(te, nxt, par, n_used, src_head, src_ahead, dst_prev, dst_own,
      h2t, w_up, w_down, b_up_p, b_down, perm)


def _combine_kernel(x2_ref, route_ref, y0_ref, y1_ref, y2_ref, y3_ref, g_ref, o_ref):
    tt = x2_ref.shape[0]
    gates = [route_ref[:, k:k + 1] for k in range(TOP_K)]
    chunks = []
    ss = jnp.zeros((tt, 1), F32)
    for c in range(ROW_CHUNKS):
        acc = x2_ref[:, c * LANES:(c + 1) * LANES]
        for k, y_ref in enumerate((y0_ref, y1_ref, y2_ref, y3_ref)):
            acc = acc + gates[k] * y_ref[pl.ds(c, tt, stride=ROW_CHUNKS), :]
        chunks.append(acc)
        ss = ss + jnp.sum(acc * acc, axis=-1, keepdims=True)
    inv = lax.rsqrt(ss / D_MODEL + RMS_EPS)
    for c in range(ROW_CHUNKS):
        o_ref[:, c * LANES:(c + 1) * LANES] = chunks[c] * inv * g_ref[:, c * LANES:(c + 1) * LANES]


def _combine(x2, route, y, g):
    t = x2.shape[0]
    nt = t // T_OUT
    plane = lambda k: pl.BlockSpec((T_OUT * ROW_CHUNKS, LANES), lambda i: (k * nt + i, 0))
    return pl.pallas_call(
        _combine_kernel,
        grid=(nt,),
        in_specs=[pl.BlockSpec((T_OUT, D_MODEL), lambda i: (i, 0)),
                  pl.BlockSpec((T_OUT, LANES), lambda i: (i, 0)),
                  plane(0), plane(1), plane(2), plane(3),
                  pl.BlockSpec((1, D_MODEL), lambda i: (0, 0))],
        out_specs=pl.BlockSpec((T_OUT, D_MODEL), lambda i: (i, 0)),
        out_shape=jax.ShapeDtypeStruct((t, D_MODEL), F32),
        compiler_params=pltpu.CompilerParams(dimension_semantics=("arbitrary",),
                                             vmem_limit_bytes=VMEM_LIMIT),
        name="combine",
    )(x2, route, y, y, y, y, g)


def _t5_bucket(dist):
    max_exact = N_BUCKETS // 2
    d = np.maximum(dist, 1).astype(np.float32)
    large = max_exact + (np.log(d / max_exact) / math.log(MAX_DISTANCE / max_exact)
                         * (N_BUCKETS - max_exact)).astype(np.int32)
    large = np.minimum(large, N_BUCKETS - 1)
    return np.where(dist < max_exact, dist, large).astype(np.int32)


def _band_tables():
    qi = np.arange(BLOCK)[:, None]
    kj = np.arange(2 * BLOCK)[None, :]
    dist = BLOCK + qi - kj
    in_band = (dist >= 0) & (dist < WINDOW)
    bucket = _t5_bucket(np.clip(dist, 0, None)).reshape(-1)
    onehot = (bucket[:, None] == np.arange(N_BUCKETS)[None, :]).astype(np.float32)
    return onehot, in_band


def _glu_perm():
    p = np.zeros((256, 256), np.float32)
    c = np.arange(LANES)
    p[2 * c, c] = 1.0
    p[2 * c + 1, LANES + c] = 1.0
    return p


def _routing_tables(eid, t, n_tiles):
    tm = T_MOE
    n_pairs = t * TOP_K
    flat = eid.reshape(-1)
    skey = jnp.sort(flat * n_pairs + jnp.arange(n_pairs, dtype=jnp.int32))
    experts = jnp.arange(N_EXPERTS, dtype=jnp.int32)
    counts = jnp.sum((flat[:, None] == experts[None, :]).astype(jnp.int32), axis=0)
    tiles_e = (counts + tm - 1) // tm
    tile_end = jnp.cumsum(tiles_e)
    tile_start = tile_end - tiles_e
    row_start = jnp.cumsum(counts) - counts
    n_used = tile_end[-1]
    ti = jnp.arange(n_tiles, dtype=jnp.int32)
    expert_of = lambda tile: jnp.sum((tile[..., None] >= tile_end).astype(jnp.int32), axis=-1)
    te = expert_of(jnp.minimum(ti, n_used - 1))
    nxt = expert_of(jnp.minimum(tile_end[te], n_used - 1))
    par = (jnp.cumsum((tiles_e > 0).astype(jnp.int32))[te] - 1) % 2
    rows = jnp.arange(tm, dtype=jnp.int32)[None, :]
    local = (ti - tile_start[te])[:, None] * tm + rows
    valid = (local < counts[te][:, None]) & (ti < n_used)[:, None]
    sidx = jnp.clip(row_start[te][:, None] + local, 0, n_pairs - 1)
    pair = skey[sidx] % n_pairs
    tok = pair // TOP_K
    k = pair % TOP_K
    src = jnp.where(valid, tok, 0).astype(jnp.int32)
    discard = lambda tile: n_pairs + (tile % RING)[:, None] * tm + rows
    dst = jnp.where(valid, k * t + tok, discard(ti)).astype(jnp.int32)
    src_ahead = jnp.concatenate([src[2:], src[-1:], src[-1:]], axis=0)
    dst_prev = jnp.concatenate([discard(jnp.full((1,), RING - 1, jnp.int32)), dst[:-1]], axis=0)
    shape3 = lambda a: a.reshape(-1, 1, tm)
    return (shape3(src_ahead), shape3(dst_prev), shape3(dst), shape3(src[:2]),
            te.astype(jnp.int32), nxt.astype(jnp.int32), par.astype(jnp.int32),
            n_used.reshape(1).astype(jnp.int32))


def kernel(x, norm1_g, w_in, pool_w, pool_scale, w_pool_proj, sinks, w_attn_proj, w_out, norm2_g,
           w_router, b_router, w_up, b_up, w_down, b_down, rel_bias, final_g):
    batch, seq, d = x.shape
    t = batch * seq
    assert w_in.shape[0] == 1, "single-layer block"
    l = 0
    onehot, in_band = _band_tables()
    perm = jnp.asarray(_glu_perm(), BF16)
    n_tiles = t * TOP_K // T_MOE + N_EXPERTS
    assert n_tiles % RING == 0
    y_rows = (TOP_K * t + RING * T_MOE) * ROW_CHUNKS

    x2d = x.reshape(t, d)
    u, q, kv, gates = _inproj(x2d, norm1_g[l][None, :], w_in[l].astype(BF16))

    bias = jnp.dot(jnp.asarray(onehot), rel_bias.astype(F32), precision=lax.Precision.HIGHEST)
    bias = jnp.transpose(bias.reshape(BLOCK, 2 * BLOCK, N_Q_HEADS), (2, 0, 1))
    bias = jnp.where(in_band[None], bias, NEG)
    wr = jnp.pad(w_router[l], ((0, 0), (0, LANES - N_EXPERTS)))
    wr_hi = wr.astype(BF16)
    wr_lo = (wr - wr_hi.astype(F32)).astype(BF16)
    wr3 = jnp.concatenate([wr_hi, wr_hi, wr_lo], axis=0)
    br = jnp.pad(b_router[l], (0, LANES - N_EXPERTS), constant_values=-jnp.inf)[None, :]
    x2, h2t, route = _mixer(
        x2d, u, q, kv, gates, sinks[l], pool_w[l].astype(BF16), pool_scale[l][None, :],
        w_pool_proj[l].astype(BF16), bias, w_attn_proj[l].astype(BF16),
        w_out[l].astype(BF16), norm2_g[l][None, :], wr3, br, batch, seq)

    eid = route[:, TOP_K:2 * TOP_K].astype(jnp.int32)
    tables = _routing_tables(eid, t, n_tiles)
    b_up_p = (b_up[l].reshape(N_EXPERTS, 2 * D_FF // 256, LANES, 2)
              .transpose(0, 1, 3, 2).reshape(N_EXPERTS, 2 * D_FF))
    y = _moe(h2t, tables, w_up[l], b_up_p, w_down[l], b_down[l], perm, n_tiles, y_rows)

    out = _combine(x2, route, y, final_g[None, :])
    return out.reshape(batch, seq, d)
```
